```python
import math
import jax, jax.numpy as jnp
from jax import lax
import numpy as np

D_MODEL = 1024
BATCH = 8
SEQ = 4096
DEPTH = 1

RET_HEADS = 8
RET_DK = 64
RET_DV = 64
RET_WIDTH = RET_HEADS * RET_DV
CHUNK = 128
ROPE_BASE = 10000.0
DIFF_HEADS = 4
DIFF_DK = 64
DIFF_DV = 2 * DIFF_DK
DIFF_WIDTH = DIFF_HEADS * DIFF_DV
Q_BLOCK = 128
REL_BUCKETS = 32
REL_MAX_DIST = 128
PEER_HEADS = 8
PEER_NKEYS = 128
PEER_EXPERTS = PEER_NKEYS * PEER_NKEYS
PEER_DQ = 256
PEER_TOPK = 16
TOKEN_BLOCK = 128
EPS = 1e-6

MIX_WIDTH = RET_WIDTH + DIFF_WIDTH
IN_SIZES = [RET_HEADS * RET_DK, RET_HEADS * RET_DK, RET_WIDTH, RET_WIDTH,
            DIFF_HEADS * 2 * DIFF_DK, DIFF_HEADS * 2 * DIFF_DK, DIFF_WIDTH]
IN_WIDTH = sum(IN_SIZES)
IN_OFFSETS = [int(o) for o in np.cumsum(IN_SIZES)[:-1]]

kernel_name = "hymba_retention_diffattn_peer"


def rmsnorm(x, g):
    xf = x.astype(jnp.float32)
    y = xf * lax.rsqrt(jnp.mean(xf * xf, axis=-1, keepdims=True) + EPS)
    return (y * g.astype(jnp.float32)).astype(x.dtype)


def rotary(x, pos):
    half = x.shape[-1] // 2
    freqs = ROPE_BASE ** (-jnp.arange(half, dtype=jnp.float32) / half)
    ang = pos.astype(jnp.float32)[:, None] * freqs[None, :]
    cos, sin = jnp.cos(ang).astype(x.dtype), jnp.sin(ang).astype(x.dtype)
    x1, x2 = x[..., :half], x[..., half:]
    return jnp.concatenate([x1 * cos - x2 * sin, x2 * cos + x1 * sin], axis=-1)


def retention(q, k, v):
    B, H, S, dk = q.shape
    dv = v.shape[-1]
    C = S // CHUNK
    log_gamma = jnp.log(1.0 - 2.0 ** (-5.0 - jnp.arange(H, dtype=jnp.float32)))
    idx = jnp.arange(CHUNK, dtype=jnp.float32)
    dist = idx[:, None] - idx[None, :]
    decay = jnp.where(dist >= 0, jnp.exp(log_gamma[:, None, None] * jnp.maximum(dist, 0.0)), 0.0)
    qc = q.astype(jnp.float32).reshape(B, H, C, CHUNK, dk)
    kc = k.astype(jnp.float32).reshape(B, H, C, CHUNK, dk)
    vc = v.astype(jnp.float32).reshape(B, H, C, CHUNK, dv)
    scores = jnp.einsum('bhcld,bhcmd->bhclm', qc, kc) * decay[None, :, None]
    inner = jnp.einsum('bhclm,bhcme->bhcle', scores, vc)
    k_decay = jnp.exp(log_gamma[:, None] * (CHUNK - 1 - idx)[None, :])
    kv = jnp.einsum('bhcld,bhcle->cbhde', kc * k_decay[None, :, None, :, None], vc)
    chunk_decay = jnp.exp(log_gamma * CHUNK)[None, :, None, None]

    def step(R, kv_c):
        return R * chunk_decay + kv_c, R

    _, R_prev = lax.scan(step, jnp.zeros((B, H, dk, dv), jnp.float32), kv)
    q_decay = jnp.exp(log_gamma[:, None] * (idx + 1.0)[None, :])
    cross = jnp.einsum('bhcld,cbhde->bhcle', qc, R_prev) * q_decay[None, :, None, :, None]
    return (inner + cross).reshape(B, H, S, dv)


def t5_bucket(n):
    n = jnp.maximum(n, 0)
    max_exact = REL_BUCKETS // 2
    large = max_exact + (jnp.log(jnp.maximum(n, 1).astype(jnp.float32) / max_exact)
                         / math.log(REL_MAX_DIST / max_exact)
                         * (REL_BUCKETS - max_exact)).astype(jnp.int32)
    large = jnp.minimum(large, REL_BUCKETS - 1)
    return jnp.where(n < max_exact, n, large)


def diff_attention(q1, q2, k1, k2, v, lam, rel_bias):
    B, H, S, _ = q1.shape
    dv = v.shape[-1]
    nb = S // Q_BLOCK
    kpos = jnp.arange(S, dtype=jnp.int32)
    scale = DIFF_DK ** -0.5
    vf = v.astype(jnp.float32)

    def block(i):
        start = i * Q_BLOCK
        qb1 = lax.dynamic_slice_in_dim(q1, start, Q_BLOCK, axis=2)
        qb2 = lax.dynamic_slice_in_dim(q2, start, Q_BLOCK, axis=2)
        qpos = start + jnp.arange(Q_BLOCK, dtype=jnp.int32)
        rel = qpos[:, None] - kpos[None, :]
        bias = jnp.take(rel_bias.astype(jnp.float32), t5_bucket(rel), axis=0)
        bias = jnp.transpose(bias, (2, 0, 1))[None]
        mask = (rel >= 0)[None, None]

        def probs(qb, k):
            s = jnp.einsum('bhqd,bhkd->bhqk', qb, k).astype(jnp.float32) * scale + bias
            return jax.nn.softmax(jnp.where(mask, s, -jnp.inf), axis=-1)

        p = probs(qb1, k1) - lam * probs(qb2, k2)
        return jnp.einsum('bhqk,bhkd->bhqd', p, vf)

    out = lax.map(block, jnp.arange(nb))
    return jnp.transpose(out, (1, 2, 0, 3, 4)).reshape(B, H, S, dv)


def peer(h, w_pq, sub_keys, w_down, w_up):
    B, S, D = h.shape
    hb = h.reshape((B * S) // TOKEN_BLOCK, TOKEN_BLOCK, D)

    def block(hx):
        q = (hx @ w_pq).reshape(TOKEN_BLOCK, PEER_HEADS, 2, PEER_DQ // 2)
        s = jnp.einsum('nhpd,hpkd->nhpk', q, sub_keys).astype(jnp.float32)
        sv, si = lax.top_k(s, PEER_TOPK)
        cand = (sv[:, :, 0, :, None] + sv[:, :, 1, None, :]).reshape(TOKEN_BLOCK, PEER_HEADS, PEER_TOPK * PEER_TOPK)
        cidx = (si[:, :, 0, :, None] * PEER_NKEYS + si[:, :, 1, None, :]).reshape(TOKEN_BLOCK, PEER_HEADS, PEER_TOPK * PEER_TOPK)
        top, pos = lax.top_k(cand, PEER_TOPK)
        experts = jnp.take_along_axis(cidx, pos, axis=-1)
        g = jax.nn.softmax(top, axis=-1)
        u = w_down[experts]
        a = jnp.einsum('nd,nhkd->nhk', hx, u).astype(jnp.float32)
        act = (jax.nn.gelu(a, approximate=False) * g).astype(hx.dtype)
        vsel = w_up[experts]
        return jnp.einsum('nhk,nhkd->nd', act, vsel)

    return lax.map(block, hb).reshape(B, S, D)


def setup_inputs(seed: int = 0) -> dict:
    key = jax.random.key(seed)
    ks = jax.random.split(key, 20)
    f32 = jnp.float32
    nrm = lambda k, shape, s: jax.random.normal(k, shape, f32) * s
    return {
        "x": nrm(ks[0], (BATCH, SEQ, D_MODEL), 1.0),
        "norm_mix": 1.0 + nrm(ks[1], (DEPTH, D_MODEL), 0.01),
        "w_in": nrm(ks[2], (DEPTH, D_MODEL, IN_WIDTH), D_MODEL ** -0.5),
        "ret_gn": 1.0 + nrm(ks[3], (DEPTH, RET_WIDTH), 0.01),
        "diff_lambda_q1": nrm(ks[4], (DEPTH, DIFF_DK), 0.1),
        "diff_lambda_k1": nrm(ks[5], (DEPTH, DIFF_DK), 0.1),
        "diff_lambda_q2": nrm(ks[6], (DEPTH, DIFF_DK), 0.1),
        "diff_lambda_k2": nrm(ks[7], (DEPTH, DIFF_DK), 0.1),
        "diff_subln": 1.0 + nrm(ks[8], (DEPTH, DIFF_DV), 0.01),
        "rel_bias": nrm(ks[9], (REL_BUCKETS, DIFF_HEADS), 0.5),
        "w_out": nrm(ks[10], (DEPTH, MIX_WIDTH, D_MODEL), MIX_WIDTH ** -0.5),
        "norm_ffn": 1.0 + nrm(ks[11], (DEPTH, D_MODEL), 0.01),
        "peer_query": nrm(ks[12], (DEPTH, D_MODEL, PEER_HEADS * PEER_DQ), D_MODEL ** -0.5),
        "peer_keys": nrm(ks[13], (DEPTH, PEER_HEADS, 2, PEER_NKEYS, PEER_DQ // 2), (PEER_DQ // 2) ** -0.5),
        "peer_down": nrm(ks[14], (DEPTH, PEER_EXPERTS, D_MODEL), D_MODEL ** -0.5),
        "peer_up": nrm(ks[15], (DEPTH, PEER_EXPERTS, D_MODEL), PEER_HEADS ** -0.5),
        "norm_final": 1.0 + nrm(ks[16], (D_MODEL,), 0.01),
    }


def reference(x, norm_mix, w_in, ret_gn, diff_lambda_q1, diff_lambda_k1, diff_lambda_q2,
              diff_lambda_k2, diff_subln, rel_bias, w_out, norm_ffn, peer_query, peer_keys,
              peer_down, peer_up, norm_final):
    B, S, _ = x.shape
    pos = jnp.arange(S, dtype=jnp.int32)

    def heads(t, H):
        return t.reshape(B, S, H, -1).transpose(0, 2, 1, 3)

    for l in range(DEPTH):
        lambda_init = 0.8 - 0.6 * math.exp(-0.3 * l)
        h = rmsnorm(x, norm_mix[l])
        proj = h @ w_in[l]
        rq, rk, rv, rg, dq, dk, dvv = jnp.split(proj, IN_OFFSETS, axis=-1)

        rq = rotary(heads(rq, RET_HEADS), pos)
        rk = rotary(heads(rk, RET_HEADS), pos) * (RET_DK ** -0.5)
        ro = retention(rq, rk, heads(rv, RET_HEADS))
        mu = jnp.mean(ro, axis=-1, keepdims=True)
        var = jnp.mean(jnp.square(ro - mu), axis=-1, keepdims=True)
        ro = ((ro - mu) * lax.rsqrt(var + EPS)).transpose(0, 2, 1, 3).reshape(B, S, RET_WIDTH)
        ro = jax.nn.silu(rg.astype(jnp.float32)) * (ro * ret_gn[l].astype(jnp.float32))

        dq = dq.reshape(B, S, DIFF_HEADS, 2, DIFF_DK)
        dk = dk.reshape(B, S, DIFF_HEADS, 2, DIFF_DK)
        q1 = dq[..., 0, :].transpose(0, 2, 1, 3)
        q2 = dq[..., 1, :].transpose(0, 2, 1, 3)
        k1 = dk[..., 0, :].transpose(0, 2, 1, 3)
        k2 = dk[..., 1, :].transpose(0, 2, 1, 3)
        lam = (jnp.exp(jnp.sum(diff_lambda_q1[l].astype(jnp.float32) * diff_lambda_k1[l].astype(jnp.float32)))
               - jnp.exp(jnp.sum(diff_lambda_q2[l].astype(jnp.float32) * diff_lambda_k2[l].astype(jnp.float32)))
               + lambda_init)
        do = diff_attention(q1, q2, k1, k2, heads(dvv, DIFF_HEADS), lam, rel_bias)
        do = do * lax.rsqrt(jnp.mean(do * do, axis=-1, keepdims=True) + EPS)
        do = do * diff_subln[l].astype(jnp.float32) * (1.0 - lambda_init)
        do = do.transpose(0, 2, 1, 3).reshape(B, S, DIFF_WIDTH)

        mix = jnp.concatenate([ro, do], axis=-1).astype(x.dtype)
        x = x + mix @ w_out[l]

        h2 = rmsnorm(x, norm_ffn[l])
        x = x + peer(h2, peer_query[l], peer_keys[l], peer_down[l], peer_up[l]).astype(x.dtype)

    return rmsnorm(x, norm_final)
```

```python
import functools
import math

import jax
import jax.numpy as jnp
import numpy as np
from jax import lax
from jax.experimental import pallas as pl
from jax.experimental.pallas import tpu as pltpu

F32 = jnp.float32
BF16 = jnp.bfloat16

EPS = 1e-6
ROPE_BASE = 10000.0
RET_HEADS = 8
RET_DK = 64
RET_DV = 64
CHUNK = 128
DIFF_HEADS = 4
DIFF_DK = 64
DIFF_DV = 128
REL_BUCKETS = 32
REL_MAX_DIST = 128
PEER_HEADS = 8
PEER_NKEYS = 128
PEER_TOPK = 16
LAMBDA_INIT = 0.8 - 0.6 * math.exp(-0.3 * 0)

LANES = 128
SUBLANES = 8
VMEM_LIMIT = 56 * 1024 * 1024
NEG = -1e30

TN_PROJ = 512
TS_RET = 512
TQ = 256
TK = 256
TN_PREP = 256
TN_PEER = 512
EB_PEER = 512


def _dot(a, b):
    return jnp.dot(a, b, preferred_element_type=F32)


def _dot_nt(a, b):
    return lax.dot_general(a, b, (((1,), (1,)), ((), ())), preferred_element_type=F32)


def _params(*sem):
    return pltpu.CompilerParams(dimension_semantics=sem, vmem_limit_bytes=VMEM_LIMIT)


def _in_proj_body(x_ref, g_ref, wf_ref, wk_ref, cos_ref, sin_ref,
                  rq_ref, rk_ref, rv_ref, rg_ref, dq_ref, dk_ref, dv_ref):
    xf = x_ref[...]
    ms = jnp.mean(xf * xf, axis=-1, keepdims=True)
    h = ((xf * lax.rsqrt(ms + EPS)) * g_ref[...]).astype(BF16)
    cos = cos_ref[...]
    sin = sin_ref[...]
    half = RET_DK // 2

    def rotary(t):
        outs = []
        for hh in range(RET_HEADS):
            x1 = t[hh * RET_DK:hh * RET_DK + half]
            x2 = t[hh * RET_DK + half:(hh + 1) * RET_DK]
            outs.append(x1 * cos - x2 * sin)
            outs.append(x2 * cos + x1 * sin)
        return jnp.concatenate(outs, axis=0)

    w = 512
    rq_ref[...] = rotary(_dot_nt(wf_ref[0 * w:1 * w], h)).astype(BF16)
    rk = rotary(_dot_nt(wf_ref[1 * w:2 * w], h)) * (RET_DK ** -0.5)
    rk_ref[...] = rk.T.astype(BF16)
    rv_ref[...] = _dot_nt(wf_ref[2 * w:3 * w], h).astype(BF16)
    rg_ref[...] = _dot_nt(wf_ref[3 * w:4 * w], h)
    dq_ref[...] = (_dot_nt(wf_ref[4 * w:5 * w], h) * (DIFF_DK ** -0.5)).astype(BF16)
    dv = _dot_nt(wf_ref[5 * w:6 * w], h).astype(BF16)
    for t in range(dv_ref.shape[0]):
        dv_ref[t] = dv[:, t * TK:(t + 1) * TK]
    dk_ref[...] = _dot(h, wk_ref[...]).astype(BF16)


def _in_proj(x2d, g, wfT, wk, cosT, sinT, seq):
    n, d = x2d.shape
    tn = TN_PROJ
    nt = n // tn
    spt = seq // tn
    feat = lambda i: (0, i)
    tok = lambda i: (i, 0)
    const = lambda i: (0, 0)
    return pl.pallas_call(
        _in_proj_body,
        grid=(nt,),
        in_specs=[
            pl.BlockSpec((tn, d), tok),
            pl.BlockSpec((1, d), const),
            pl.BlockSpec(wfT.shape, const),
            pl.BlockSpec(wk.shape, const),
            pl.BlockSpec((RET_DK // 2, tn), lambda i: (0, i % spt)),
            pl.BlockSpec((RET_DK // 2, tn), lambda i: (0, i % spt)),
        ],
        out_specs=[
            pl.BlockSpec((512, tn), feat),
            pl.BlockSpec((tn, 512), tok),
            pl.BlockSpec((512, tn), feat),
            pl.BlockSpec((512, tn), feat),
            pl.BlockSpec((512, tn), feat),
            pl.BlockSpec((tn, 512), tok),
            pl.BlockSpec((tn // TK, 512, TK), lambda i: (i, 0, 0)),
        ],
        out_shape=[
            jax.ShapeDtypeStruct((512, n), BF16),
            jax.ShapeDtypeStruct((n, 512), BF16),
            jax.ShapeDtypeStruct((512, n), BF16),
            jax.ShapeDtypeStruct((512, n), F32),
            jax.ShapeDtypeStruct((512, n), BF16),
            jax.ShapeDtypeStruct((n, 512), BF16),
            jax.ShapeDtypeStruct((n // TK, 512, TK), BF16),
        ],
        compiler_params=_params("parallel"),
        name="in_proj",
    )(x2d, g, wfT, wk, cosT, sinT)


def _retention_body(rq_ref, rk_ref, rv_ref, rg_ref, dec_ref, kd_ref, qd_ref, cdm_ref, bm_ref,
                    gn_ref, ro_ref, state_ref):
    @pl.when(pl.program_id(1) == 0)
    def _():
        state_ref[...] = jnp.zeros_like(state_ref)

    ts = rq_ref.shape[1]
    row = lax.broadcasted_iota(jnp.int32, (2 * RET_DK, CHUNK), 0)
    top = row < RET_DK
    bm = bm_ref[...]
    for c in range(ts // CHUNK):
        sl = slice(c * CHUNK, (c + 1) * CHUNK)
        k_c = rk_ref[sl, :]
        kdec = (k_c.astype(F32) * kd_ref[...]).astype(BF16)
        for pp in range(RET_HEADS // 2):
            rows = slice(pp * 128, (pp + 1) * 128)
            qT = rq_ref[rows, sl]
            vT = rv_ref[rows, sl]
            zero = jnp.zeros_like(qT)
            qblk = jnp.concatenate([jnp.where(top, qT, zero), jnp.where(top, zero, qT)], axis=1)
            sc = _dot(k_c[:, rows], qblk)
            p0 = (sc[:, :CHUNK] * dec_ref[2 * pp]).astype(BF16)
            p1 = (sc[:, CHUNK:] * dec_ref[2 * pp + 1]).astype(BF16)
            inner = jnp.concatenate([_dot(vT[:RET_DV], p0), _dot(vT[RET_DV:], p1)], axis=0)
            st = state_ref[pp]
            cross = _dot(st.astype(BF16), qT) * qd_ref[rows, :]
            o = inner + cross
            kv = _dot(vT, kdec[:, rows])
            state_ref[pp] = st * cdm_ref[pp] + kv * bm
            rg = rg_ref[rows, sl]
            gate = rg * jax.nn.sigmoid(rg)
            outs = []
            for hh in range(2):
                oh = o[hh * RET_DV:(hh + 1) * RET_DV]
                mu = jnp.mean(oh, axis=0, keepdims=True)
                var = jnp.mean(jnp.square(oh - mu), axis=0, keepdims=True)
                outs.append((oh - mu) * lax.rsqrt(var + EPS))
            on = jnp.concatenate(outs, axis=0)
            ro_ref[rows, sl] = (gate * (on * gn_ref[rows, :])).astype(BF16)


def _retention(rqT, rk, rvT, rgT, tabs, gn, batch, seq):
    n = rqT.shape[1]
    ts = TS_RET
    spb = seq // ts
    feat = lambda b, s: (0, b * spb + s)
    tok = lambda b, s: (b * spb + s, 0)
    c2 = lambda b, s: (0, 0)
    c3 = lambda b, s: (0, 0, 0)
    dec, kd, qd, cdm, bm = tabs
    return pl.pallas_call(
        _retention_body,
        grid=(batch, spb),
        in_specs=[
            pl.BlockSpec((512, ts), feat),
            pl.BlockSpec((ts, 512), tok),
            pl.BlockSpec((512, ts), feat),
            pl.BlockSpec((512, ts), feat),
            pl.BlockSpec(dec.shape, c3),
            pl.BlockSpec(kd.shape, c2),
            pl.BlockSpec(qd.shape, c2),
            pl.BlockSpec(cdm.shape, c3),
            pl.BlockSpec(bm.shape, c2),
            pl.BlockSpec(gn.shape, c2),
        ],
        out_specs=pl.BlockSpec((512, ts), feat),
        out_shape=jax.ShapeDtypeStruct((512, n), BF16),
        scratch_shapes=[pltpu.VMEM((RET_HEADS // 2, 128, 128), F32)],
        compiler_params=_params("parallel", "arbitrary"),
        name="retention",
    )(rqT, rk, rvT, rgT, dec, kd, qd, cdm, bm, gn)


def _retention_tables():
    lg = np.log(1.0 - 2.0 ** (-5.0 - np.arange(RET_HEADS, dtype=np.float64)))
    idx = np.arange(CHUNK, dtype=np.float64)
    dist = idx[None, :] - idx[:, None]
    dec = np.where(dist >= 0, np.exp(lg[:, None, None] * np.maximum(dist, 0.0)), 0.0)
    kd = np.repeat(np.exp(lg[:, None] * (CHUNK - 1 - idx)[None, :]), RET_DK, axis=0).T
    qd = np.repeat(np.exp(lg[:, None] * (idx + 1.0)[None, :]), RET_DV, axis=0)
    cd = np.exp(lg * CHUNK)
    bm = np.zeros((128, 128))
    bm[:64, :64] = 1.0
    bm[64:, 64:] = 1.0
    cdm = np.zeros((RET_HEADS // 2, 128, 128))
    for pp in range(RET_HEADS // 2):
        cdm[pp, :64, :64] = cd[2 * pp]
        cdm[pp, 64:, 64:] = cd[2 * pp + 1]
    return tuple(jnp.asarray(t, F32) for t in (dec, kd, qd, cdm, bm))


def _diff_attn_body(lam_ref, dq_ref, k_ref, v_ref, bias_ref, sub_ref, do_ref,
                    qblk_ref, m_ref, l_ref, acc_ref):
    qi = pl.program_id(2)
    qT = dq_ref[...]
    row = lax.broadcasted_iota(jnp.int32, qT.shape, 0)
    zero = jnp.zeros_like(qT)
    qblk_ref[:, :TQ] = jnp.where(row < DIFF_DK, qT, zero)
    qblk_ref[:, TQ:] = jnp.where(row < DIFF_DK, zero, qT)
    m_ref[...] = jnp.full_like(m_ref, NEG)
    l_ref[...] = jnp.zeros_like(l_ref)
    acc_ref[...] = jnp.zeros_like(acc_ref)

    def body(j, carry):
        k = k_ref[pl.ds(pl.multiple_of(j * TK, TK), TK), :]
        s = _dot(k, qblk_ref[...])
        bt = bias_ref[jnp.minimum(qi - j, 2)]
        s = s + jnp.concatenate([bt, bt], axis=1)
        m_old = m_ref[...]
        m_new = jnp.maximum(m_old, jnp.max(s, axis=0, keepdims=True))
        p = jnp.exp(s - m_new)
        alpha = jnp.exp(m_old - m_new)
        l_ref[...] = alpha * l_ref[...] + jnp.sum(p, axis=0, keepdims=True)
        m_ref[...] = m_new
        acc_ref[...] = acc_ref[...] * alpha + _dot(v_ref[j], p.astype(BF16))
        return carry

    lax.fori_loop(0, qi + 1, body, 0)

    acc = acc_ref[...]
    inv = 1.0 / l_ref[...]
    o = acc[:, :TQ] * inv[:, :TQ] - lam_ref[0] * (acc[:, TQ:] * inv[:, TQ:])
    o = o * lax.rsqrt(jnp.mean(o * o, axis=0, keepdims=True) + EPS)
    do_ref[...] = (o * sub_ref[...] * (1.0 - LAMBDA_INIT)).astype(BF16)


def _diff_attn(lam, dqT, dk, dvT, bias, subln, batch, seq):
    n = dqT.shape[1]
    nq = seq // TQ
    nk = seq // TK
    return pl.pallas_call(
        _diff_attn_body,
        grid=(batch, DIFF_HEADS, nq),
        in_specs=[
            pl.BlockSpec(memory_space=pltpu.SMEM),
            pl.BlockSpec((2 * DIFF_DK, TQ), lambda b, h, q: (h, b * nq + q)),
            pl.BlockSpec((seq, 2 * DIFF_DK), lambda b, h, q: (b, h)),
            pl.BlockSpec((nk, DIFF_DV, TK), lambda b, h, q: (b, h, 0)),
            pl.BlockSpec((None, 3, TK, TQ), lambda b, h, q: (h, 0, 0, 0)),
            pl.BlockSpec((DIFF_DV, 1), lambda b, h, q: (0, 0)),
        ],
        out_specs=pl.BlockSpec((DIFF_DV, TQ), lambda b, h, q: (h, b * nq + q)),
        out_shape=jax.ShapeDtypeStruct((DIFF_HEADS * DIFF_DV, n), BF16),
        scratch_shapes=[
            pltpu.VMEM((2 * DIFF_DK, 2 * TQ), BF16),
            pltpu.VMEM((1, 2 * TQ), F32),
            pltpu.VMEM((1, 2 * TQ), F32),
            pltpu.VMEM((DIFF_DV, 2 * TQ), F32),
        ],
        compiler_params=_params("parallel", "parallel", "arbitrary"),
        name="diff_attn",
    )(lam, dqT, dk, dvT, bias, subln)


def _t5_bucket(nrel):
    nrel = jnp.maximum(nrel, 0)
    max_exact = REL_BUCKETS // 2
    large = max_exact + (jnp.log(jnp.maximum(nrel, 1).astype(F32) / max_exact)
                         / math.log(REL_MAX_DIST / max_exact)
                         * (REL_BUCKETS - max_exact)).astype(jnp.int32)
    large = jnp.minimum(large, REL_BUCKETS - 1)
    return jnp.where(nrel < max_exact, nrel, large)


def _bias_tiles(rel_bias):
    kpos = jnp.arange(TK, dtype=jnp.int32)[:, None]
    qpos = jnp.arange(TQ, dtype=jnp.int32)[None, :]
    tiles = []
    for off in range(3):
        rel = off * TQ + qpos - kpos
        b = jnp.take(rel_bias.astype(F32), _t5_bucket(rel), axis=0)
        b = jnp.where((rel >= 0)[:, :, None], b, NEG)
        tiles.append(jnp.transpose(b, (2, 0, 1)))
    return jnp.stack(tiles, axis=1)


def _sort_network(nel):
    pairs = []

    def merge(lo, hi, r):
        step = r * 2
        if step < hi - lo:
            merge(lo, hi, step)
            merge(lo + r, hi, step)
            for i in range(lo + r, hi - r, step):
                pairs.append((i, i + r))
        else:
            pairs.append((lo, lo + r))

    def sort(lo, hi):
        if hi - lo >= 1:
            mid = lo + (hi - lo) // 2
            sort(lo, mid)
            sort(mid + 1, hi)
            merge(lo, hi, 1)

    sort(0, nel - 1)
    return pairs


_SORT16 = _sort_network(PEER_TOPK)


def _compare_exchange(v, i, j):
    hi = jnp.maximum(v[i], v[j])
    lo = jnp.minimum(v[i], v[j])
    v[i] = hi
    v[j] = lo


def _bitonic_merge(v):
    d = len(v) // 2
    while d >= 1:
        for i in range(len(v)):
            if i & d == 0:
                _compare_exchange(v, i, i + d)
        d //= 2


def _top16_desc(s3):
    v = [s3[i] for i in range(PEER_TOPK)]
    for i, j in _SORT16:
        _compare_exchange(v, i, j)
    for shift in (4, 2, 1):
        other = [pltpu.roll(a, shift, 0) for a in v]
        v = [jnp.maximum(v[i], other[PEER_TOPK - 1 - i]) for i in range(PEER_TOPK)]
        _bitonic_merge(v)
    return v


_CAND = [(a, b) for a in range(PEER_TOPK) for b in range(PEER_TOPK) if (a + 1) * (b + 1) <= PEER_TOPK]


def _peer_prep_body(x_ref, ro_ref, do_ref, wo_ref, g2_ref, wq_ref, keys_ref,
                    x1_ref, h2_ref, rank_ref, e1_ref, cnt_ref, c_ref):
    tn = x_ref.shape[0]
    xT = x_ref[...].T
    half = ro_ref.shape[0]
    x1 = xT + _dot(wo_ref[:, :half], ro_ref[...]) + _dot(wo_ref[:, half:], do_ref[...])
    x1_ref[...] = x1
    ms = jnp.mean(x1 * x1, axis=0, keepdims=True)
    h2 = ((x1 * lax.rsqrt(ms + EPS)) * g2_ref[...]).astype(BF16)
    h2_ref[...] = h2
    qp = _dot(wq_ref[...], h2).astype(BF16)

    nsub = PEER_NKEYS // SUBLANES
    sub = lax.broadcasted_iota(jnp.int32, (SUBLANES, tn), 0)
    scores = []
    tops = []
    w = [[jnp.zeros((SUBLANES, tn), F32) for _ in range(PEER_TOPK)] for _ in range(2)]
    for h in range(PEER_HEADS):
        sc_h, top_h = [], []
        for p in range(2):
            r0 = (h * 2 + p) * PEER_NKEYS
            s = _dot(keys_ref[h * 2 + p], qp[r0:r0 + PEER_NKEYS])
            s3 = s.reshape(nsub, SUBLANES, tn)
            t = _top16_desc(s3)
            sc_h.append(s3)
            top_h.append(t)
            for a in range(PEER_TOPK):
                w[p][a] = jnp.where(sub == h, t[a], w[p][a])
        scores.append(sc_h)
        tops.append(top_h)

    cand = {ab: w[0][ab[0]] + w[1][ab[1]] for ab in _CAND}
    work = dict(cand)
    for _ in range(PEER_TOPK - 1):
        mx = functools.reduce(jnp.maximum, work.values())
        work = {ab: jnp.where(c == mx, -jnp.inf, c) for ab, c in work.items()}
    tau = functools.reduce(jnp.maximum, work.values())
    cmax = cand[(0, 0)]
    z = jnp.zeros_like(tau)
    cnt = [jnp.zeros_like(tau) for _ in range(PEER_TOPK)]
    for (a, b), c in cand.items():
        sel = c >= tau
        z = z + jnp.where(sel, jnp.exp(c - cmax), 0.0)
        cnt[a] = cnt[a] + jnp.where(sel, 1.0, 0.0)
    zinv = 1.0 / z

    for h in range(PEER_HEADS):
        s0, s1 = scores[h]
        t0, t1 = tops[h]
        bc = lambda a: jnp.broadcast_to(a[h:h + 1, :], (SUBLANES, tn))[None]
        cnt3 = jnp.zeros_like(s0)
        for a in range(PEER_TOPK - 1, -1, -1):
            cnt3 = jnp.where(s0 >= t0[a][None], bc(cnt[a]), cnt3)
        c3 = jnp.exp(s0 - t0[0][None]) * bc(zinv)
        rank3 = jnp.zeros_like(s1)
        for b in range(PEER_TOPK):
            rank3 = rank3 + jnp.where(s1 < t1[b][None], 1.0, 0.0)
        e3 = jnp.exp(s1 - t1[0][None])
        rows = slice(h * PEER_NKEYS, (h + 1) * PEER_NKEYS)
        cnt_ref[rows, :] = cnt3.reshape(PEER_NKEYS, tn)
        c_ref[rows, :] = c3.reshape(PEER_NKEYS, tn)
        rank_ref[rows, :] = rank3.reshape(PEER_NKEYS, tn).astype(BF16)
        e1_ref[rows, :] = e3.reshape(PEER_NKEYS, tn).astype(BF16)


def _peer_prep(x2d, roT, doT, woT, g2, wqT, keys):
    n, d = x2d.shape
    tn = TN_PREP
    hk = PEER_HEADS * PEER_NKEYS
    feat = lambda i: (0, i)
    c2 = lambda i: (0, 0)
    return pl.pallas_call(
        _peer_prep_body,
        grid=(n // tn,),
        in_specs=[
            pl.BlockSpec((tn, d), lambda i: (i, 0)),
            pl.BlockSpec((roT.shape[0], tn), feat),
            pl.BlockSpec((doT.shape[0], tn), feat),
            pl.BlockSpec(woT.shape, c2),
            pl.BlockSpec(g2.shape, c2),
            pl.BlockSpec(wqT.shape, c2),
            pl.BlockSpec(keys.shape, lambda i: (0, 0, 0)),
        ],
        out_specs=[
            pl.BlockSpec((d, tn), feat),
            pl.BlockSpec((d, tn), feat),
            pl.BlockSpec((hk, tn), feat),
            pl.BlockSpec((hk, tn), feat),
            pl.BlockSpec((hk, tn), feat),
            pl.BlockSpec((hk, tn), feat),
        ],
        out_shape=[
            jax.ShapeDtypeStruct((d, n), F32),
            jax.ShapeDtypeStruct((d, n), BF16),
            jax.ShapeDtypeStruct((hk, n), BF16),
            jax.ShapeDtypeStruct((hk, n), BF16),
            jax.ShapeDtypeStruct((hk, n), F32),
            jax.ShapeDtypeStruct((hk, n), F32),
        ],
        compiler_params=_params("parallel"),
        name="peer_prep",
    )(x2d, roT, doT, woT, g2, wqT, keys)


def _gelu(a):
    return 0.5 * a * (1.0 + lax.erf(a * (2.0 ** -0.5)))


def _peer_dense_body(h2_ref, wd_ref, wu_ref, rank_ref, e1_ref, cnt_ref, c_ref, x1_ref, gf_ref,
                     out_ref, acc_ref):
    e = pl.program_id(1)

    @pl.when(e == 0)
    def _():
        acc_ref[...] = jnp.zeros_like(acc_ref)

    a_t = _dot(wd_ref[...], h2_ref[...])
    nblk = wd_ref.shape[0] // PEER_NKEYS
    acts = []
    for ii in range(nblk):
        i = e * nblk + ii
        gate = None
        for h in range(PEER_HEADS):
            rows = slice(h * PEER_NKEYS, (h + 1) * PEER_NKEYS)
            cnt_row = cnt_ref[pl.ds(h * PEER_NKEYS + i, 1), :].astype(BF16)
            c_row = c_ref[pl.ds(h * PEER_NKEYS + i, 1), :].astype(BF16)
            term = jnp.where(rank_ref[rows, :] < cnt_row, e1_ref[rows, :] * c_row,
                             jnp.zeros((), BF16))
            gate = term if gate is None else gate + term
        a = a_t[ii * PEER_NKEYS:(ii + 1) * PEER_NKEYS]
        acts.append((_gelu(a) * gate.astype(F32)).astype(BF16))
    act = jnp.concatenate(acts, axis=0)
    acc_ref[...] += _dot(wu_ref[...], act)

    @pl.when(e == pl.num_programs(1) - 1)
    def _():
        x2 = x1_ref[...] + acc_ref[...]
        ms = jnp.mean(x2 * x2, axis=0, keepdims=True)
        y = (x2 * lax.rsqrt(ms + EPS)) * gf_ref[...]
        out_ref[...] = y.T


def _peer_dense(h2T, wd, wuT, rankT, e1T, cntT, cT, x1T, gf):
    d, n = h2T.shape
    ne = wd.shape[0]
    tn = TN_PEER
    eb = EB_PEER
    hk = PEER_HEADS * PEER_NKEYS
    feat = lambda t, e: (0, t)
    return pl.pallas_call(
        _peer_dense_body,
        grid=(n // tn, ne // eb),
        in_specs=[
            pl.BlockSpec((d, tn), feat),
            pl.BlockSpec((eb, d), lambda t, e: (e, 0)),
            pl.BlockSpec((d, eb), lambda t, e: (0, e)),
            pl.BlockSpec((hk, tn), feat),
            pl.BlockSpec((hk, tn), feat),
            pl.BlockSpec((hk, tn), feat),
            pl.BlockSpec((hk, tn), feat),
            pl.BlockSpec((d, tn), feat),
            pl.BlockSpec(gf.shape, lambda t, e: (0, 0)),
        ],
        out_specs=pl.BlockSpec((tn, d), lambda t, e: (t, 0)),
        out_shape=jax.ShapeDtypeStruct((n, d), F32),
        scratch_shapes=[pltpu.VMEM((d, tn), F32)],
        compiler_params=_params("parallel", "arbitrary"),
        name="peer_dense",
    )(h2T, wd, wuT, rankT, e1T, cntT, cT, x1T, gf)


def kernel(x, norm_mix, w_in, ret_gn, diff_lambda_q1, diff_lambda_k1, diff_lambda_q2, diff_lambda_k2,
           diff_subln, rel_bias, w_out, norm_ffn, peer_query, peer_keys, peer_down, peer_up, norm_final):
    batch, seq, d = x.shape
    n = batch * seq
    x2d = x.reshape(n, d)
    l = 0

    w = w_in[l]
    wfT = jnp.concatenate([w[:, :2560], w[:, 3072:]], axis=1).T.astype(BF16)
    wk = w[:, 2560:3072].astype(BF16)
    pos = jnp.arange(seq, dtype=F32)
    half = RET_DK // 2
    freqs = ROPE_BASE ** (-jnp.arange(half, dtype=F32) / half)
    ang = freqs[:, None] * pos[None, :]
    cosT, sinT = jnp.cos(ang), jnp.sin(ang)
    lam = (jnp.exp(jnp.sum(diff_lambda_q1[l].astype(F32) * diff_lambda_k1[l].astype(F32)))
           - jnp.exp(jnp.sum(diff_lambda_q2[l].astype(F32) * diff_lambda_k2[l].astype(F32)))
           + LAMBDA_INIT).reshape(1)
    woT = w_out[l].T.astype(BF16)
    wqT = peer_query[l].T.astype(BF16)
    keys = peer_keys[l].reshape(PEER_HEADS * 2, PEER_NKEYS, -1).astype(BF16)
    wd = peer_down[l].astype(BF16)
    wuT = peer_up[l].T.astype(BF16)

    rqT, rk, rvT, rgT, dqT, dk, dvT = _in_proj(
        x2d, norm_mix[l].reshape(1, d), wfT, wk, cosT, sinT, seq)
    roT = _retention(rqT, rk, rvT, rgT, _retention_tables(), ret_gn[l].reshape(-1, 1), batch, seq)
    doT = _diff_attn(lam, dqT, dk, dvT, _bias_tiles(rel_bias), diff_subln[l].reshape(-1, 1), batch, seq)
    x1T, h2T, rankT, e1T, cntT, cT = _peer_prep(
        x2d, roT, doT, woT, norm_ffn[l].reshape(-1, 1), wqT, keys)
    out = _peer_dense(h2T, wd, wuT, rankT, e1T, cntT, cT, x1T, norm_final.reshape(-1, 1))
    return out.reshape(batch, seq, d)
```

```python
import functools
import math

import jax
import jax.numpy as jnp
import numpy as np
from jax import lax
from jax.experimental import pallas as pl
from jax.experimental.pallas import tpu as pltpu

F32 = jnp.float32
BF16 = jnp.bfloat16

EPS = 1e-6
ROPE_BASE = 10000.0
RET_HEADS = 8
RET_DK = 64
RET_DV = 64
CHUNK = 128
DIFF_HEADS = 4
DIFF_DK = 64
DIFF_DV = 128
REL_BUCKETS = 32
REL_MAX_DIST = 128
PEER_HEADS = 8
PEER_NKEYS = 128
PEER_TOPK = 16
LAMBDA_INIT = 0.8 - 0.6 * math.exp(-0.3 * 0)

LANES = 128
SUBLANES = 8
VMEM_LIMIT = 56 * 1024 * 1024
NEG = -1e30

TN_PROJ = 512
TS_RET = 512
TQ = 512
TK = 512
TN_PREP = 256
TN_PEER = 512
EB_PEER = 512


def _dot(a, b):
    return jnp.dot(a, b, preferred_element_type=F32)


def _dot_nt(a, b):
    return lax.dot_general(a, b, (((1,), (1,)), ((), ())), preferred_element_type=F32)


def _params(*sem):
    return pltpu.CompilerParams(dimension_semantics=sem, vmem_limit_bytes=VMEM_LIMIT)


def _in_proj_body(x_ref, g_ref, wf_ref, wk_ref, cos_ref, sin_ref,
                  rq_ref, rk_ref, rv_ref, rg_ref, dq_ref, dk_ref, dv_ref):
    xf = x_ref[...]
    ms = jnp.mean(xf * xf, axis=-1, keepdims=True)
    h = ((xf * lax.rsqrt(ms + EPS)) * g_ref[...]).astype(BF16)
    cos = cos_ref[...]
    sin = sin_ref[...]
    half = RET_DK // 2

    def rotary(t):
        outs = []
        for hh in range(RET_HEADS):
            x1 = t[hh * RET_DK:hh * RET_DK + half]
            x2 = t[hh * RET_DK + half:(hh + 1) * RET_DK]
            outs.append(x1 * cos - x2 * sin)
            outs.append(x2 * cos + x1 * sin)
        return jnp.concatenate(outs, axis=0)

    w = 512
    rq_ref[...] = rotary(_dot_nt(wf_ref[0 * w:1 * w], h)).astype(BF16)
    rk = rotary(_dot_nt(wf_ref[1 * w:2 * w], h)) * (RET_DK ** -0.5)
    rk_ref[...] = rk.T.astype(BF16)
    rv_ref[...] = _dot_nt(wf_ref[2 * w:3 * w], h).astype(BF16)
    rg_ref[...] = _dot_nt(wf_ref[3 * w:4 * w], h)
    dq_ref[...] = (_dot_nt(wf_ref[4 * w:5 * w], h) * (DIFF_DK ** -0.5)).astype(BF16)
    dv = _dot_nt(wf_ref[5 * w:6 * w], h).astype(BF16)
    for t in range(dv_ref.shape[0]):
        dv_ref[t] = dv[:, t * TK:(t + 1) * TK]
    dk_ref[...] = _dot(h, wk_ref[...]).astype(BF16)


def _in_proj(x2d, g, wfT, wk, cosT, sinT, seq):
    n, d = x2d.shape
    tn = TN_PROJ
    nt = n // tn
    spt = seq // tn
    feat = lambda i: (0, i)
    tok = lambda i: (i, 0)
    const = lambda i: (0, 0)
    return pl.pallas_call(
        _in_proj_body,
        grid=(nt,),
        in_specs=[
            pl.BlockSpec((tn, d), tok),
            pl.BlockSpec((1, d), const),
            pl.BlockSpec(wfT.shape, const),
            pl.BlockSpec(wk.shape, const),
            pl.BlockSpec((RET_DK // 2, tn), lambda i: (0, i % spt)),
            pl.BlockSpec((RET_DK // 2, tn), lambda i: (0, i % spt)),
        ],
        out_specs=[
            pl.BlockSpec((512, tn), feat),
            pl.BlockSpec((tn, 512), tok),
            pl.BlockSpec((512, tn), feat),
            pl.BlockSpec((512, tn), feat),
            pl.BlockSpec((512, tn), feat),
            pl.BlockSpec((tn, 512), tok),
            pl.BlockSpec((tn // TK, 512, TK), lambda i: (i, 0, 0)),
        ],
        out_shape=[
            jax.ShapeDtypeStruct((512, n), BF16),
            jax.ShapeDtypeStruct((n, 512), BF16),
            jax.ShapeDtypeStruct((512, n), BF16),
            jax.ShapeDtypeStruct((512, n), F32),
            jax.ShapeDtypeStruct((512, n), BF16),
            jax.ShapeDtypeStruct((n, 512), BF16),
            jax.ShapeDtypeStruct((n // TK, 512, TK), BF16),
        ],
        compiler_params=_params("parallel"),
        name="in_proj",
    )(x2d, g, wfT, wk, cosT, sinT)


def _retention_body(rq_ref, rk_ref, rv_ref, rg_ref, dec_ref, kd_ref, qd_ref, cdm_ref, bm_ref,
                    gn_ref, ro_ref, state_ref):
    @pl.when(pl.program_id(1) == 0)
    def _():
        state_ref[...] = jnp.zeros_like(state_ref)

    ts = rq_ref.shape[1]
    row = lax.broadcasted_iota(jnp.int32, (2 * RET_DK, CHUNK), 0)
    top = row < RET_DK
    bm = bm_ref[...]
    for c in range(ts // CHUNK):
        sl = slice(c * CHUNK, (c + 1) * CHUNK)
        k_c = rk_ref[sl, :]
        kdec = (k_c.astype(F32) * kd_ref[...]).astype(BF16)
        for pp in range(RET_HEADS // 2):
            rows = slice(pp * 128, (pp + 1) * 128)
            qT = rq_ref[rows, sl]
            vT = rv_ref[rows, sl]
            zero = jnp.zeros_like(qT)
            qblk = jnp.concatenate([jnp.where(top, qT, zero), jnp.where(top, zero, qT)], axis=1)
            sc = _dot(k_c[:, rows], qblk)
            p0 = (sc[:, :CHUNK] * dec_ref[2 * pp]).astype(BF16)
            p1 = (sc[:, CHUNK:] * dec_ref[2 * pp + 1]).astype(BF16)
            inner = jnp.concatenate([_dot(vT[:RET_DV], p0), _dot(vT[RET_DV:], p1)], axis=0)
            st = state_ref[pp]
            cross = _dot(st.astype(BF16), qT) * qd_ref[rows, :]
            o = inner + cross
            kv = _dot(vT, kdec[:, rows])
            state_ref[pp] = st * cdm_ref[pp] + kv * bm
            rg = rg_ref[rows, sl]
            gate = rg * jax.nn.sigmoid(rg)
            outs = []
            for hh in range(2):
                oh = o[hh * RET_DV:(hh + 1) * RET_DV]
                mu = jnp.mean(oh, axis=0, keepdims=True)
                var = jnp.mean(jnp.square(oh - mu), axis=0, keepdims=True)
                outs.append((oh - mu) * lax.rsqrt(var + EPS))
            on = jnp.concatenate(outs, axis=0)
            ro_ref[rows, sl] = (gate * (on * gn_ref[rows, :])).astype(BF16)


def _retention(rqT, rk, rvT, rgT, tabs, gn, batch, seq):
    n = rqT.shape[1]
    ts = TS_RET
    spb = seq // ts
    feat = lambda b, s: (0, b * spb + s)
    tok = lambda b, s: (b * spb + s, 0)
    c2 = lambda b, s: (0, 0)
    c3 = lambda b, s: (0, 0, 0)
    dec, kd, qd, cdm, bm = tabs
    return pl.pallas_call(
        _retention_body,
        grid=(batch, spb),
        in_specs=[
            pl.BlockSpec((512, ts), feat),
            pl.BlockSpec((ts, 512), tok),
            pl.BlockSpec((512, ts), feat),
            pl.BlockSpec((512, ts), feat),
            pl.BlockSpec(dec.shape, c3),
            pl.BlockSpec(kd.shape, c2),
            pl.BlockSpec(qd.shape, c2),
            pl.BlockSpec(cdm.shape, c3),
            pl.BlockSpec(bm.shape, c2),
            pl.BlockSpec(gn.shape, c2),
        ],
        out_specs=pl.BlockSpec((512, ts), feat),
        out_shape=jax.ShapeDtypeStruct((512, n), BF16),
        scratch_shapes=[pltpu.VMEM((RET_HEADS // 2, 128, 128), F32)],
        compiler_params=_params("parallel", "arbitrary"),
        name="retention",
    )(rqT, rk, rvT, rgT, dec, kd, qd, cdm, bm, gn)


def _retention_tables():
    lg = np.log(1.0 - 2.0 ** (-5.0 - np.arange(RET_HEADS, dtype=np.float64)))
    idx = np.arange(CHUNK, dtype=np.float64)
    dist = idx[None, :] - idx[:, None]
    dec = np.where(dist >= 0, np.exp(lg[:, None, None] * np.maximum(dist, 0.0)), 0.0)
    kd = np.repeat(np.exp(lg[:, None] * (CHUNK - 1 - idx)[None, :]), RET_DK, axis=0).T
    qd = np.repeat(np.exp(lg[:, None] * (idx + 1.0)[None, :]), RET_DV, axis=0)
    cd = np.exp(lg * CHUNK)
    bm = np.zeros((128, 128))
    bm[:64, :64] = 1.0
    bm[64:, 64:] = 1.0
    cdm = np.zeros((RET_HEADS // 2, 128, 128))
    for pp in range(RET_HEADS // 2):
        cdm[pp, :64, :64] = cd[2 * pp]
        cdm[pp, 64:, 64:] = cd[2 * pp + 1]
    return tuple(jnp.asarray(t, F32) for t in (dec, kd, qd, cdm, bm))


def _diff_attn_body(lam_ref, dq_ref, k_ref, v_ref, bias_ref, sub_ref, do_ref,
                    qblk_ref, m_ref, l_ref, acc_ref):
    qi = pl.program_id(2)
    qT = dq_ref[...]
    row = lax.broadcasted_iota(jnp.int32, qT.shape, 0)
    zero = jnp.zeros_like(qT)
    qblk_ref[:, :TQ] = jnp.where(row < DIFF_DK, qT, zero)
    qblk_ref[:, TQ:] = jnp.where(row < DIFF_DK, zero, qT)
    m_ref[...] = jnp.full_like(m_ref, NEG)
    l_ref[...] = jnp.zeros_like(l_ref)
    acc_ref[...] = jnp.zeros_like(acc_ref)

    def body(j, carry):
        k = k_ref[pl.ds(pl.multiple_of(j * TK, TK), TK), :]
        s = _dot(k, qblk_ref[...])
        bt = bias_ref[jnp.minimum(qi - j, 2)]
        s = s + jnp.concatenate([bt, bt], axis=1)
        m_old = m_ref[...]
        m_new = jnp.maximum(m_old, jnp.max(s, axis=0, keepdims=True))
        p = jnp.exp(s - m_new)
        alpha = jnp.exp(m_old - m_new)
        l_ref[...] = alpha * l_ref[...] + jnp.sum(p, axis=0, keepdims=True)
        m_ref[...] = m_new
        acc_ref[...] = acc_ref[...] * alpha + _dot(v_ref[j], p.astype(BF16))
        return carry

    lax.fori_loop(0, qi + 1, body, 0)

    acc = acc_ref[...]
    inv = 1.0 / l_ref[...]
    o = acc[:, :TQ] * inv[:, :TQ] - lam_ref[0] * (acc[:, TQ:] * inv[:, TQ:])
    o = o * lax.rsqrt(jnp.mean(o * o, axis=0, keepdims=True) + EPS)
    do_ref[...] = (o * sub_ref[...] * (1.0 - LAMBDA_INIT)).astype(BF16)


def _diff_attn(lam, dqT, dk, dvT, bias, subln, batch, seq):
    n = dqT.shape[1]
    nq = seq // TQ
    nk = seq // TK
    return pl.pallas_call(
        _diff_attn_body,
        grid=(batch, DIFF_HEADS, nq),
        in_specs=[
            pl.BlockSpec(memory_space=pltpu.SMEM),
            pl.BlockSpec((2 * DIFF_DK, TQ), lambda b, h, q: (h, b * nq + q)),
            pl.BlockSpec((seq, 2 * DIFF_DK), lambda b, h, q: (b, h)),
            pl.BlockSpec((nk, DIFF_DV, TK), lambda b, h, q: (b, h, 0)),
            pl.BlockSpec((None, 3, TK, TQ), lambda b, h, q: (h, 0, 0, 0)),
            pl.BlockSpec((DIFF_DV, 1), lambda b, h, q: (0, 0)),
        ],
        out_specs=pl.BlockSpec((DIFF_DV, TQ), lambda b, h, q: (h, b * nq + q)),
        out_shape=jax.ShapeDtypeStruct((DIFF_HEADS * DIFF_DV, n), BF16),
        scratch_shapes=[
            pltpu.VMEM((2 * DIFF_DK, 2 * TQ), BF16),
            pltpu.VMEM((1, 2 * TQ), F32),
            pltpu.VMEM((1, 2 * TQ), F32),
            pltpu.VMEM((DIFF_DV, 2 * TQ), F32),
        ],
        compiler_params=_params("parallel", "parallel", "arbitrary"),
        name="diff_attn",
    )(lam, dqT, dk, dvT, bias, subln)


def _t5_bucket(nrel):
    nrel = jnp.maximum(nrel, 0)
    max_exact = REL_BUCKETS // 2
    large = max_exact + (jnp.log(jnp.maximum(nrel, 1).astype(F32) / max_exact)
                         / math.log(REL_MAX_DIST / max_exact)
                         * (REL_BUCKETS - max_exact)).astype(jnp.int32)
    large = jnp.minimum(large, REL_BUCKETS - 1)
    return jnp.where(nrel < max_exact, nrel, large)


def _bias_tiles(rel_bias):
    span = TK + TQ + 1
    tiles = []
    for off in range(3):
        rel = off * TQ - TK + jnp.arange(span, dtype=jnp.int32)
        g = jnp.take(rel_bias.astype(F32), _t5_bucket(rel), axis=0)
        g = jnp.where((rel >= 0)[:, None], g, NEG).T
        t = jnp.tile(g, (1, TK))[:, :TK * (span - 1)].reshape(-1, TK, span - 1)
        tiles.append(t[:, :, TK:TK + TQ])
    return jnp.stack(tiles, axis=1)


def _sort_network(nel):
    pairs = []

    def merge(lo, hi, r):
        step = r * 2
        if step < hi - lo:
            merge(lo, hi, step)
            merge(lo + r, hi, step)
            for i in range(lo + r, hi - r, step):
                pairs.append((i, i + r))
        else:
            pairs.append((lo, lo + r))

    def sort(lo, hi):
        if hi - lo >= 1:
            mid = lo + (hi - lo) // 2
            sort(lo, mid)
            sort(mid + 1, hi)
            merge(lo, hi, 1)

    sort(0, nel - 1)
    return pairs


_SORT16 = _sort_network(PEER_TOPK)


def _compare_exchange(v, i, j):
    hi = jnp.maximum(v[i], v[j])
    lo = jnp.minimum(v[i], v[j])
    v[i] = hi
    v[j] = lo


def _bitonic_merge(v):
    d = len(v) // 2
    while d >= 1:
        for i in range(len(v)):
            if i & d == 0:
                _compare_exchange(v, i, i + d)
        d //= 2


def _top16_desc(s3):
    v = [s3[i] for i in range(PEER_TOPK)]
    for i, j in _SORT16:
        _compare_exchange(v, i, j)
    for shift in (4, 2, 1):
        other = [pltpu.roll(a, shift, 0) for a in v]
        v = [jnp.maximum(v[i], other[PEER_TOPK - 1 - i]) for i in range(PEER_TOPK)]
        _bitonic_merge(v)
    return v


_CAND = [(a, b) for a in range(PEER_TOPK) for b in range(PEER_TOPK) if (a + 1) * (b + 1) <= PEER_TOPK]


def _peer_prep_body(x_ref, ro_ref, do_ref, wo_ref, g2_ref, wq_ref, keys_ref,
                    x1_ref, h2_ref, rank_ref, e1_ref, cnt_ref, c_ref):
    tn = x_ref.shape[0]
    xT = x_ref[...].T
    half = ro_ref.shape[0]
    x1 = xT + _dot(wo_ref[:, :half], ro_ref[...]) + _dot(wo_ref[:, half:], do_ref[...])
    x1_ref[...] = x1
    ms = jnp.mean(x1 * x1, axis=0, keepdims=True)
    h2 = ((x1 * lax.rsqrt(ms + EPS)) * g2_ref[...]).astype(BF16)
    h2_ref[...] = h2
    qp = _dot(wq_ref[...], h2).astype(BF16)

    nsub = PEER_NKEYS // SUBLANES
    sub = lax.broadcasted_iota(jnp.int32, (SUBLANES, tn), 0)
    scores = []
    tops = []
    w = [[jnp.zeros((SUBLANES, tn), F32) for _ in range(PEER_TOPK)] for _ in range(2)]
    for h in range(PEER_HEADS):
        sc_h, top_h = [], []
        for p in range(2):
            r0 = (h * 2 + p) * PEER_NKEYS
            s = _dot(keys_ref[h * 2 + p], qp[r0:r0 + PEER_NKEYS])
            s3 = s.reshape(nsub, SUBLANES, tn)
            t = _top16_desc(s3)
            sc_h.append(s3)
            top_h.append(t)
            for a in range(PEER_TOPK):
                w[p][a] = jnp.where(sub == h, t[a], w[p][a])
        scores.append(sc_h)
        tops.append(top_h)

    cand = {ab: w[0][ab[0]] + w[1][ab[1]] for ab in _CAND}
    work = dict(cand)
    for _ in range(PEER_TOPK - 1):
        mx = functools.reduce(jnp.maximum, work.values())
        work = {ab: jnp.where(c == mx, -jnp.inf, c) for ab, c in work.items()}
    tau = functools.reduce(jnp.maximum, work.values())
    cmax = cand[(0, 0)]
    z = jnp.zeros_like(tau)
    cnt = [jnp.zeros_like(tau) for _ in range(PEER_TOPK)]
    for (a, b), c in cand.items():
        sel = c >= tau
        z = z + jnp.where(sel, jnp.exp(c - cmax), 0.0)
        cnt[a] = cnt[a] + jnp.where(sel, 1.0, 0.0)
    zinv = 1.0 / z

    for h in range(PEER_HEADS):
        s0, s1 = scores[h]
        t0, t1 = tops[h]
        bc = lambda a: jnp.broadcast_to(a[h:h + 1, :], (SUBLANES, tn))[None]
        cnt3 = jnp.zeros_like(s0)
        for a in range(PEER_TOPK - 1, -1, -1):
            cnt3 = jnp.where(s0 >= t0[a][None], bc(cnt[a]), cnt3)
        c3 = jnp.exp(s0 - t0[0][None]) * bc(zinv)
        rank3 = jnp.zeros_like(s1)
        for b in range(PEER_TOPK):
            rank3 = rank3 + jnp.where(s1 < t1[b][None], 1.0, 0.0)
        e3 = jnp.exp(s1 - t1[0][None])
        rows = slice(h * PEER_NKEYS, (h + 1) * PEER_NKEYS)
        cnt_ref[rows, :] = cnt3.reshape(PEER_NKEYS, tn)
        c_ref[rows, :] = c3.reshape(PEER_NKEYS, tn)
        rank_ref[rows, :] = rank3.reshape(PEER_NKEYS, tn).astype(BF16)
        e1_ref[rows, :] = e3.reshape(PEER_NKEYS, tn).astype(BF16)


def _peer_prep(x2d, roT, doT, woT, g2, wqT, keys):
    n, d = x2d.shape
    tn = TN_PREP
    hk = PEER_HEADS * PEER_NKEYS
    feat = lambda i: (0, i)
    c2 = lambda i: (0, 0)
    return pl.pallas_call(
        _peer_prep_body,
        grid=(n // tn,),
        in_specs=[
            pl.BlockSpec((tn, d), lambda i: (i, 0)),
            pl.BlockSpec((roT.shape[0], tn), feat),
            pl.BlockSpec((doT.shape[0], tn), feat),
            pl.BlockSpec(woT.shape, c2),
            pl.BlockSpec(g2.shape, c2),
            pl.BlockSpec(wqT.shape, c2),
            pl.BlockSpec(keys.shape, lambda i: (0, 0, 0)),
        ],
        out_specs=[
            pl.BlockSpec((d, tn), feat),
            pl.BlockSpec((d, tn), feat),
            pl.BlockSpec((hk, tn), feat),
            pl.BlockSpec((hk, tn), feat),
            pl.BlockSpec((hk, tn), feat),
            pl.BlockSpec((hk, tn), feat),
        ],
        out_shape=[
            jax.ShapeDtypeStruct((d, n), F32),
            jax.ShapeDtypeStruct((d, n), BF16),
            jax.ShapeDtypeStruct((hk, n), BF16),
            jax.ShapeDtypeStruct((hk, n), BF16),
            jax.ShapeDtypeStruct((hk, n), F32),
            jax.ShapeDtypeStruct((hk, n), F32),
        ],
        compiler_params=_params("parallel"),
        name="peer_prep",
    )(x2d, roT, doT, woT, g2, wqT, keys)


def _gelu(a):
    return 0.5 * a * (1.0 + lax.erf(a * (2.0 ** -0.5)))


def _peer_dense_body(h2_ref, wd_ref, wu_ref, rank_ref, e1_ref, cnt_ref, c_ref, x1_ref, gf_ref,
                     out_ref, acc_ref):
    e = pl.program_id(1)

    @pl.when(e == 0)
    def _():
        acc_ref[...] = jnp.zeros_like(acc_ref)

    a_t = _dot(wd_ref[...], h2_ref[...])
    nblk = wd_ref.shape[0] // PEER_NKEYS
    acts = []
    for ii in range(nblk):
        i = e * nblk + ii
        gate = None
        for h in range(PEER_HEADS):
            rows = slice(h * PEER_NKEYS, (h + 1) * PEER_NKEYS)
            cnt_row = cnt_ref[pl.ds(h * PEER_NKEYS + i, 1), :].astype(BF16)
            c_row = c_ref[pl.ds(h * PEER_NKEYS + i, 1), :].astype(BF16)
            term = jnp.where(rank_ref[rows, :] < cnt_row, e1_ref[rows, :] * c_row,
                             jnp.zeros((), BF16))
            gate = term if gate is None else gate + term
        a = a_t[ii * PEER_NKEYS:(ii + 1) * PEER_NKEYS]
        acts.append((_gelu(a) * gate.astype(F32)).astype(BF16))
    act = jnp.concatenate(acts, axis=0)
    acc_ref[...] += _dot(wu_ref[...], act)

    @pl.when(e == pl.num_programs(1) - 1)
    def _():
        x2 = x1_ref[...] + acc_ref[...]
        ms = jnp.mean(x2 * x2, axis=0, keepdims=True)
        y = (x2 * lax.rsqrt(ms + EPS)) * gf_ref[...]
        out_ref[...] = y.T


def _peer_dense(h2T, wd, wuT, rankT, e1T, cntT, cT, x1T, gf):
    d, n = h2T.shape
    ne = wd.shape[0]
    tn = TN_PEER
    eb = EB_PEER
    hk = PEER_HEADS * PEER_NKEYS
    feat = lambda t, e: (0, t)
    return pl.pallas_call(
        _peer_dense_body,
        grid=(n // tn, ne // eb),
        in_specs=[
            pl.BlockSpec((d, tn), feat),
            pl.BlockSpec((eb, d), lambda t, e: (e, 0)),
            pl.BlockSpec((d, eb), lambda t, e: (0, e)),
            pl.BlockSpec((hk, tn), feat),
            pl.BlockSpec((hk, tn), feat),
            pl.BlockSpec((hk, tn), feat),
            pl.BlockSpec((hk, tn), feat),
            pl.BlockSpec((d, tn), feat),
            pl.BlockSpec(gf.shape, lambda t, e: (0, 0)),
        ],
        out_specs=pl.BlockSpec((tn, d), lambda t, e: (t, 0)),
        out_shape=jax.ShapeDtypeStruct((n, d), F32),
        scratch_shapes=[pltpu.VMEM((d, tn), F32)],
        compiler_params=_params("parallel", "arbitrary"),
        name="peer_dense",
    )(h2T, wd, wuT, rankT, e1T, cntT, cT, x1T, gf)


def kernel(x, norm_mix, w_in, ret_gn, diff_lambda_q1, diff_lambda_k1, diff_lambda_q2, diff_lambda_k2,
           diff_subln, rel_bias, w_out, norm_ffn, peer_query, peer_keys, peer_down, peer_up, norm_final):
    batch, seq, d = x.shape
    n = batch * seq
    x2d = x.reshape(n, d)
    l = 0

    w = w_in[l]
    wfT = jnp.concatenate([w[:, :2560], w[:, 3072:]], axis=1).T.astype(BF16)
    wk = w[:, 2560:3072].astype(BF16)
    pos = jnp.arange(seq, dtype=F32)
    half = RET_DK // 2
    freqs = ROPE_BASE ** (-jnp.arange(half, dtype=F32) / half)
    ang = freqs[:, None] * pos[None, :]
    cosT, sinT = jnp.cos(ang), jnp.sin(ang)
    lam = (jnp.exp(jnp.sum(diff_lambda_q1[l].astype(F32) * diff_lambda_k1[l].astype(F32)))
           - jnp.exp(jnp.sum(diff_lambda_q2[l].astype(F32) * diff_lambda_k2[l].astype(F32)))
           + LAMBDA_INIT).reshape(1)
    woT = w_out[l].T.astype(BF16)
    wqT = peer_query[l].T.astype(BF16)
    keys = peer_keys[l].reshape(PEER_HEADS * 2, PEER_NKEYS, -1).astype(BF16)
    wd = peer_down[l].astype(BF16)
    wuT = peer_up[l].T.astype(BF16)

    rqT, rk, rvT, rgT, dqT, dk, dvT = _in_proj(
        x2d, norm_mix[l].reshape(1, d), wfT, wk, cosT, sinT, seq)
    roT = _retention(rqT, rk, rvT, rgT, _retention_tables(), ret_gn[l].reshape(-1, 1), batch, seq)
    doT = _diff_attn(lam, dqT, dk, dvT, _bias_tiles(rel_bias), diff_subln[l].reshape(-1, 1), batch, seq)
    x1T, h2T, rankT, e1T, cntT, cT = _peer_prep(
        x2d, roT, doT, woT, norm_ffn[l].reshape(-1, 1), wqT, keys)
    out = _peer_dense(h2T, wd, wuT, rankT, e1T, cntT, cT, x1T, norm_final.reshape(-1, 1))
    return out.reshape(batch, seq, d)
```
